```python
import math
import jax, jax.numpy as jnp
from jax import lax
import numpy as np

D_MODEL = 4096
BATCH = 4
SEQ = 4096
DEPTH = 1

N_META = 16
EPS = 1e-6
D_SSM = 1024
SSM_GROUP = 16
N_GROUPS = D_SSM // SSM_GROUP
STATE = 64
DT_MIN = 1e-3
DT_MAX = 1e-1
N_HEADS = 16
HEAD_DIM = 64
V_DIM = 2 * HEAD_DIM
QK_W = N_HEADS * 2 * HEAD_DIM
D_ATT = N_HEADS * V_DIM
Q_BLOCK = 128
PAD = (-N_META) % Q_BLOCK
N_BUCKETS = 32
MAX_DISTANCE = 128
NEG_INF = -1e30
D_FF = -(-8 * D_MODEL // (3 * 256)) * 256
IN_SIZES = (D_SSM, QK_W, QK_W, D_ATT, D_MODEL, D_MODEL)
IN_SPLITS = tuple(int(s) for s in np.cumsum(IN_SIZES)[:-1])
D_IN = sum(IN_SIZES)

kernel_name = "hybrid_s5_diffattn_block"


def rms_norm(x, g):
    xf = x.astype(jnp.float32)
    y = xf * lax.rsqrt(jnp.mean(xf * xf, axis=-1, keepdims=True) + EPS)
    return (y * g.astype(jnp.float32)).astype(x.dtype)


def t5_bucket(n):
    max_exact = N_BUCKETS // 2
    nf = jnp.maximum(n, max_exact).astype(jnp.float32)
    log_b = max_exact + (jnp.log(nf / max_exact) / math.log(MAX_DISTANCE / max_exact)
                         * (N_BUCKETS - max_exact)).astype(jnp.int32)
    return jnp.where(n < max_exact, n, jnp.minimum(log_b, N_BUCKETS - 1))


def _ssm_combine(e1, e2):
    a1, b1 = e1
    a2, b2 = e2
    return a1 * a2, a2 * b1 + b2


def s5_branch(u, lam_re, lam_im, log_dt, b_re, b_im, c_re, c_im, d_skip, w_glu, b_glu):
    f32 = jnp.float32
    bsz, seq_len, _ = u.shape
    lam = lax.complex(lam_re.astype(f32), lam_im.astype(f32))
    dt = jnp.exp(log_dt.astype(f32))[:, None]
    a_bar = jnp.exp(lam * dt)
    b_bar = ((a_bar - 1.0) / lam)[..., None] * lax.complex(b_re.astype(f32), b_im.astype(f32))
    c_mat = lax.complex(c_re.astype(f32), c_im.astype(f32))
    uf = u.astype(f32)
    u_g = uf.reshape(bsz, seq_len, N_GROUPS, SSM_GROUP).astype(jnp.complex64)
    bu = jnp.einsum('gpc,blgc->lbgp', b_bar, u_g)
    a_seq = jnp.broadcast_to(a_bar, (seq_len, 1, N_GROUPS, STATE))
    _, states = lax.associative_scan(_ssm_combine, (a_seq, bu), axis=0)
    y = jnp.real(jnp.einsum('gcp,lbgp->blgc', c_mat, states)).reshape(bsz, seq_len, D_SSM)
    y = y + d_skip.astype(f32) * uf
    g = jax.nn.gelu(y)
    out = g * jax.nn.sigmoid(g @ w_glu.astype(f32) + b_glu.astype(f32))
    return out.astype(u.dtype)


def diff_attention(q, k, v, lam, rel_bias):
    bsz, seq_len = q.shape[:2]
    lp = seq_len + PAD
    n_blocks = lp // Q_BLOCK
    pad = ((0, 0), (PAD, 0), (0, 0), (0, 0), (0, 0))
    q = jnp.pad(q, pad)
    k = jnp.pad(k, pad)
    v = jnp.pad(v, pad[:-1])
    q_blocks = jnp.moveaxis(q.reshape(bsz, n_blocks, Q_BLOCK, N_HEADS, 2, HEAD_DIM), 1, 0)
    key_pos = jnp.arange(lp)
    bias_hd = rel_bias[t5_bucket(key_pos)].T.astype(jnp.float32)
    scale = HEAD_DIM ** -0.5

    def block(args):
        q_blk, blk = args
        q_pos = blk * Q_BLOCK + jnp.arange(Q_BLOCK)
        dist = q_pos[:, None] - key_pos[None, :]
        mask = (dist >= 0) & (key_pos[None, :] >= PAD)
        bias = bias_hd[:, jnp.maximum(dist, 0)]
        s = jnp.einsum('bqhmd,bkhmd->bhmqk', q_blk, k).astype(jnp.float32) * scale + bias[None, :, None]
        p = jax.nn.softmax(jnp.where(mask, s, NEG_INF), axis=-1)
        attn = p[:, :, 0] - lam * p[:, :, 1]
        return jnp.einsum('bhqk,bkhe->bqhe', attn.astype(v.dtype), v)

    out = lax.map(block, (q_blocks, jnp.arange(n_blocks)))
    out = jnp.moveaxis(out, 0, 1).reshape(bsz, lp, N_HEADS, V_DIM)
    return out[:, PAD:]


def setup_inputs(seed: int = 0) -> dict:
    key = jax.random.key(seed)
    ks = jax.random.split(key, 32)
    f32 = jnp.float32

    def nrm(k, shape, scale):
        return jax.random.normal(k, shape, f32) * scale

    def gain(k, shape):
        return 1.0 + 0.01 * jax.random.normal(k, shape, f32)

    lam_im = jnp.pi * jnp.arange(STATE, dtype=f32)[None, None, :] + 0.01 * jax.random.normal(ks[13], (DEPTH, N_GROUPS, STATE), f32)
    w_branch = jnp.concatenate([nrm(ks[22], (DEPTH, D_SSM, D_MODEL), D_SSM ** -0.5),
                                nrm(ks[23], (DEPTH, D_ATT, D_MODEL), D_ATT ** -0.5)], axis=1)
    return {
        "x": nrm(ks[0], (BATCH, SEQ, D_MODEL), 1.0),
        "meta_tokens": nrm(ks[1], (N_META, D_MODEL), 1.0),
        "rel_bias": nrm(ks[2], (N_BUCKETS, N_HEADS), 0.2),
        "ln1_g": gain(ks[3], (DEPTH, D_MODEL)),
        "w_in": nrm(ks[4], (DEPTH, D_MODEL, D_IN), D_MODEL ** -0.5),
        "q_norm_g": gain(ks[5], (DEPTH, HEAD_DIM)),
        "k_norm_g": gain(ks[6], (DEPTH, HEAD_DIM)),
        "lam_q1": nrm(ks[7], (DEPTH, HEAD_DIM), 0.1),
        "lam_k1": nrm(ks[8], (DEPTH, HEAD_DIM), 0.1),
        "lam_q2": nrm(ks[9], (DEPTH, HEAD_DIM), 0.1),
        "lam_k2": nrm(ks[10], (DEPTH, HEAD_DIM), 0.1),
        "subln_g": gain(ks[11], (DEPTH, V_DIM)),
        "lam_re": -0.5 + 0.01 * jax.random.normal(ks[12], (DEPTH, N_GROUPS, STATE), f32),
        "lam_im": lam_im,
        "log_dt": jax.random.uniform(ks[14], (DEPTH, N_GROUPS), f32, math.log(DT_MIN), math.log(DT_MAX)),
        "b_re": nrm(ks[15], (DEPTH, N_GROUPS, STATE, SSM_GROUP), (2 * SSM_GROUP) ** -0.5),
        "b_im": nrm(ks[16], (DEPTH, N_GROUPS, STATE, SSM_GROUP), (2 * SSM_GROUP) ** -0.5),
        "c_re": nrm(ks[17], (DEPTH, N_GROUPS, SSM_GROUP, STATE), STATE ** -0.5),
        "c_im": nrm(ks[18], (DEPTH, N_GROUPS, SSM_GROUP, STATE), STATE ** -0.5),
        "d_skip": nrm(ks[19], (DEPTH, D_SSM), 1.0),
        "w_glu": nrm(ks[20], (DEPTH, D_SSM, D_SSM), D_SSM ** -0.5),
        "b_glu": nrm(ks[21], (DEPTH, D_SSM), 0.01),
        "w_branch": w_branch,
        "w_o": nrm(ks[24], (DEPTH, D_MODEL, D_MODEL), D_MODEL ** -0.5),
        "ln2_g": gain(ks[25], (DEPTH, D_MODEL)),
        "w_gate_up": nrm(ks[26], (DEPTH, D_MODEL, 2 * D_FF), D_MODEL ** -0.5),
        "w_down": nrm(ks[27], (DEPTH, D_FF, D_MODEL), D_FF ** -0.5),
    }


def reference(x, meta_tokens, rel_bias, ln1_g, w_in, q_norm_g, k_norm_g, lam_q1, lam_k1, lam_q2, lam_k2,
              subln_g, lam_re, lam_im, log_dt, b_re, b_im, c_re, c_im, d_skip, w_glu, b_glu,
              w_branch, w_o, ln2_g, w_gate_up, w_down):
    f32 = jnp.float32
    bsz = x.shape[0]
    h = jnp.concatenate([jnp.broadcast_to(meta_tokens[None].astype(x.dtype), (bsz, N_META, D_MODEL)), x], axis=1)
    seq_len = h.shape[1]
    for l in range(DEPTH):
        lambda_init = 0.8 - 0.6 * math.exp(-0.3 * l)
        hn = rms_norm(h, ln1_g[l])
        proj = hn @ w_in[l]
        u, q, k, v, g_ssm, g_att = jnp.split(proj, IN_SPLITS, axis=-1)
        q = rms_norm(q.reshape(bsz, seq_len, N_HEADS, 2, HEAD_DIM), q_norm_g[l])
        k = rms_norm(k.reshape(bsz, seq_len, N_HEADS, 2, HEAD_DIM), k_norm_g[l])
        v = v.reshape(bsz, seq_len, N_HEADS, V_DIM)
        lam = (jnp.exp(jnp.sum(lam_q1[l].astype(f32) * lam_k1[l].astype(f32)))
               - jnp.exp(jnp.sum(lam_q2[l].astype(f32) * lam_k2[l].astype(f32))) + lambda_init)
        o = diff_attention(q, k, v, lam, rel_bias)
        y_att = (rms_norm(o, subln_g[l]) * (1.0 - lambda_init)).reshape(bsz, seq_len, D_ATT)
        y_ssm = s5_branch(u, lam_re[l], lam_im[l], log_dt[l], b_re[l], b_im[l], c_re[l], c_im[l],
                          d_skip[l], w_glu[l], b_glu[l])
        y_a = y_ssm @ w_branch[l, :D_SSM]
        y_b = y_att @ w_branch[l, D_SSM:]
        merged = jax.nn.sigmoid(g_ssm) * y_a + jax.nn.sigmoid(g_att) * y_b
        h = h + merged @ w_o[l]
        hn = rms_norm(h, ln2_g[l])
        gate, up = jnp.split(hn @ w_gate_up[l], 2, axis=-1)
        h = h + (jax.nn.silu(gate) * up) @ w_down[l]
    return h[:, N_META:]
```

```python
import functools
import math

import jax
import jax.numpy as jnp
from jax import lax
from jax.experimental import pallas as pl
from jax.experimental.pallas import tpu as pltpu

F32 = jnp.float32
BF16 = jnp.bfloat16

N_META = 16
META_BLOCK = 128
EPS = 1e-6
SSM_GROUP = 16
SSM_CHUNK = 16
SSM_ROWS = 8
HEAD_DIM = 64
V_DIM = 2 * HEAD_DIM
N_BUCKETS = 32
MAX_DISTANCE = 128
NEG_INF = -1e30
FF_ALIGN = 1024
VMEM_LIMIT = 56 * 1024 * 1024


def _params(*semantics):
    return pltpu.CompilerParams(dimension_semantics=semantics, vmem_limit_bytes=VMEM_LIMIT)


def _pick(n, candidates):
    for c in candidates:
        if n % c == 0:
            return c
    return n


def _ln1_kernel(x_ref, meta_ref, g_ref, o_ref, *, n_x_blocks):
    i = pl.program_id(1)
    g = g_ref[...]

    def norm(v):
        ms = jnp.mean(v * v, axis=-1, keepdims=True)
        return (v * lax.rsqrt(ms + EPS) * g).astype(o_ref.dtype)

    @pl.when(i < n_x_blocks)
    def _():
        o_ref[0] = norm(x_ref[0])

    @pl.when(i == n_x_blocks)
    def _():
        o_ref[0] = jnp.zeros(o_ref.shape[1:], o_ref.dtype)
        o_ref[0, :N_META, :] = norm(meta_ref[...])


def _ln1(x, meta, g):
    bsz, s, d = x.shape
    tr = META_BLOCK
    nxb = s // tr
    return pl.pallas_call(
        functools.partial(_ln1_kernel, n_x_blocks=nxb),
        grid=(bsz, nxb + 1),
        in_specs=[
            pl.BlockSpec((1, tr, d), lambda b, i: (b, jnp.minimum(i, nxb - 1), 0)),
            pl.BlockSpec((N_META, d), lambda b, i: (0, 0)),
            pl.BlockSpec((1, d), lambda b, i: (0, 0)),
        ],
        out_specs=pl.BlockSpec((1, tr, d), lambda b, i: (b, i, 0)),
        out_shape=jax.ShapeDtypeStruct((bsz, s + META_BLOCK, d), BF16),
        compiler_params=_params("parallel", "arbitrary"),
        name="ln1",
    )(x, meta, g.reshape(1, d))


def _ln2_kernel(x_ref, g_ref, o_ref):
    v = x_ref[...]
    ms = jnp.mean(v * v, axis=-1, keepdims=True)
    o_ref[...] = (v * lax.rsqrt(ms + EPS) * g_ref[...]).astype(o_ref.dtype)


def _ln2(h, g):
    m, d = h.shape
    tr = _pick(m, (256, 128))
    return pl.pallas_call(
        _ln2_kernel,
        grid=(m // tr,),
        in_specs=[pl.BlockSpec((tr, d), lambda i: (i, 0)), pl.BlockSpec((1, d), lambda i: (0, 0))],
        out_specs=pl.BlockSpec((tr, d), lambda i: (i, 0)),
        out_shape=jax.ShapeDtypeStruct((m, d), BF16),
        compiler_params=_params("parallel"),
        name="ln2",
    )(h, g.reshape(1, d))


def _mm_kernel(a_ref, w_ref, o_ref):
    o_ref[...] = jnp.dot(a_ref[...], w_ref[...], preferred_element_type=F32).astype(o_ref.dtype)


def _in_proj(hn, w):
    m, k = hn.shape
    n = w.shape[1]
    tm = _pick(m, (1056, 528, 136))
    tn = _pick(n, (1024, 512, 256, 128))
    return pl.pallas_call(
        _mm_kernel,
        grid=(m // tm, n // tn),
        in_specs=[pl.BlockSpec((tm, k), lambda i, j: (i, 0)), pl.BlockSpec((k, tn), lambda i, j: (0, j))],
        out_specs=pl.BlockSpec((tm, tn), lambda i, j: (i, j)),
        out_shape=jax.ShapeDtypeStruct((m, n), BF16),
        compiler_params=_params("parallel", "arbitrary"),
        name="in_proj",
    )(hn, w)


def _sigmoid(x):
    return 1.0 / (1.0 + jnp.exp(-x))


def _merge_kernel(ys_ref, ya_ref, wa_ref, wb_ref, gs_ref, ga_ref, o_ref):
    a = jnp.dot(ys_ref[0], wa_ref[...], preferred_element_type=F32)
    b = jnp.dot(ya_ref[0], wb_ref[...], preferred_element_type=F32)
    gs = _sigmoid(gs_ref[0].astype(F32))
    ga = _sigmoid(ga_ref[0].astype(F32))
    o_ref[0] = (gs * a + ga * b).astype(o_ref.dtype)


def _merge(y_ssm, y_att, wa, wb, proj, gs_col, ga_col):
    bsz, s, d_ssm = y_ssm.shape
    d_att = y_att.shape[2]
    d = wa.shape[1]
    tm = _pick(s, (1024, 512, 256, 128))
    tn = _pick(math.gcd(math.gcd(d, gs_col), ga_col), (512, 256, 128))
    return pl.pallas_call(
        _merge_kernel,
        grid=(bsz, s // tm, d // tn),
        in_specs=[
            pl.BlockSpec((1, tm, d_ssm), lambda b, i, j: (b, i, 0)),
            pl.BlockSpec((1, tm, d_att), lambda b, i, j: (b, i, 0)),
            pl.BlockSpec((d_ssm, tn), lambda b, i, j: (0, j)),
            pl.BlockSpec((d_att, tn), lambda b, i, j: (0, j)),
            pl.BlockSpec((1, tm, tn), lambda b, i, j: (b, i, gs_col // tn + j)),
            pl.BlockSpec((1, tm, tn), lambda b, i, j: (b, i, ga_col // tn + j)),
        ],
        out_specs=pl.BlockSpec((1, tm, tn), lambda b, i, j: (b, i, j)),
        out_shape=jax.ShapeDtypeStruct((bsz, s, d), BF16),
        compiler_params=_params("parallel", "parallel", "arbitrary"),
        name="merge",
    )(y_ssm, y_att, wa, wb, proj, proj)


def _resid_mm_kernel(a_ref, w_ref, r_ref, o_ref):
    o_ref[...] = r_ref[...] + jnp.dot(a_ref[...], w_ref[...], preferred_element_type=F32)


def _out_proj(a, w, resid):
    m, k = a.shape
    n = w.shape[1]
    tm = _pick(m, (1024, 512, 256, 128))
    tn = _pick(n, (512, 256, 128))
    return pl.pallas_call(
        _resid_mm_kernel,
        grid=(m // tm, n // tn),
        in_specs=[
            pl.BlockSpec((tm, k), lambda i, j: (i, 0)),
            pl.BlockSpec((k, tn), lambda i, j: (0, j)),
            pl.BlockSpec((tm, tn), lambda i, j: (i, j)),
        ],
        out_specs=pl.BlockSpec((tm, tn), lambda i, j: (i, j)),
        out_shape=jax.ShapeDtypeStruct((m, n), F32),
        compiler_params=_params("parallel", "arbitrary"),
        name="out_proj",
    )(a, w, resid)


def _swiglu_kernel(a_ref, wg_ref, wu_ref, o_ref):
    a = a_ref[...]
    g = jnp.dot(a, wg_ref[...], preferred_element_type=F32)
    u = jnp.dot(a, wu_ref[...], preferred_element_type=F32)
    o_ref[...] = (g * _sigmoid(g) * u).astype(o_ref.dtype)


def _swiglu(a, wg, wu):
    m, k = a.shape
    n = wg.shape[1]
    tm = _pick(m, (1024, 512, 256, 128))
    tn = _pick(n, (512, 256, 128))
    return pl.pallas_call(
        _swiglu_kernel,
        grid=(m // tm, n // tn),
        in_specs=[
            pl.BlockSpec((tm, k), lambda i, j: (i, 0)),
            pl.BlockSpec((k, tn), lambda i, j: (0, j)),
            pl.BlockSpec((k, tn), lambda i, j: (0, j)),
        ],
        out_specs=pl.BlockSpec((tm, tn), lambda i, j: (i, j)),
        out_shape=jax.ShapeDtypeStruct((m, n), BF16),
        compiler_params=_params("parallel", "arbitrary"),
        name="swiglu",
    )(a, wg, wu)


def _down_kernel(a_ref, w_ref, r_ref, o_ref, acc_ref):
    kk = pl.program_id(2)

    @pl.when(kk == 0)
    def _():
        acc_ref[...] = r_ref[...]

    acc_ref[...] += jnp.dot(a_ref[...], w_ref[...], preferred_element_type=F32)

    @pl.when(kk == pl.num_programs(2) - 1)
    def _():
        o_ref[...] = acc_ref[...]


def _down_proj(a, w, resid):
    m, k = a.shape
    n = w.shape[1]
    tm = _pick(m, (1024, 512, 256, 128))
    tn = _pick(n, (1024, 512, 256, 128))
    tk = _pick(k, (2816, 1024, 512, 256, 128))
    return pl.pallas_call(
        _down_kernel,
        grid=(m // tm, n // tn, k // tk),
        in_specs=[
            pl.BlockSpec((tm, tk), lambda i, j, kk: (i, kk)),
            pl.BlockSpec((tk, tn), lambda i, j, kk: (kk, j)),
            pl.BlockSpec((tm, tn), lambda i, j, kk: (i, j)),
        ],
        out_specs=pl.BlockSpec((tm, tn), lambda i, j, kk: (i, j)),
        out_shape=jax.ShapeDtypeStruct((m, n), F32),
        scratch_shapes=[pltpu.VMEM((tm, tn), F32)],
        compiler_params=_params("parallel", "parallel", "arbitrary"),
        name="down_proj",
    )(a, w, resid)


def _qknorm_kernel(x_ref, g_ref, gm_ref, o_ref, *, scale):
    v = x_ref[0].astype(F32)
    sq = (v * v).astype(BF16)
    gm = gm_ref[...]
    w = gm.shape[0]
    ss = jnp.concatenate(
        [jnp.dot(sq[:, c:c + w], gm, preferred_element_type=F32) for c in range(0, v.shape[1], w)], axis=1)
    o_ref[0] = (v * lax.rsqrt(ss * (1.0 / HEAD_DIM) + EPS) * (g_ref[...] * scale)).astype(o_ref.dtype)


def _qknorm(proj, col0, width, rows, gain, scale):
    bsz = proj.shape[0]
    tr = META_BLOCK
    tc = _pick(math.gcd(col0, width), (1024, 512, 256))
    gw = min(256, tc)
    idx = jnp.arange(gw) // HEAD_DIM
    gmat = (idx[:, None] == idx[None, :]).astype(BF16)
    gain_t = jnp.tile(gain.astype(F32), tc // HEAD_DIM).reshape(1, tc)
    return pl.pallas_call(
        functools.partial(_qknorm_kernel, scale=scale),
        grid=(bsz, rows // tr, width // tc),
        in_specs=[
            pl.BlockSpec((1, tr, tc), lambda b, i, c: (b, i, col0 // tc + c)),
            pl.BlockSpec((1, tc), lambda b, i, c: (0, 0)),
            pl.BlockSpec((gw, gw), lambda b, i, c: (0, 0)),
        ],
        out_specs=pl.BlockSpec((1, tr, tc), lambda b, i, c: (b, i, c)),
        out_shape=jax.ShapeDtypeStruct((bsz, rows, width), BF16),
        compiler_params=_params("parallel", "parallel", "arbitrary"),
        name="qknorm",
    )(proj, gain_t, gmat)


def _t5_bucket(n):
    max_exact = N_BUCKETS // 2
    nf = jnp.maximum(n, max_exact).astype(F32)
    log_b = max_exact + (jnp.log(nf / max_exact) / math.log(MAX_DISTANCE / max_exact)
                         * (N_BUCKETS - max_exact)).astype(jnp.int32)
    return jnp.where(n < max_exact, n, jnp.minimum(log_b, N_BUCKETS - 1))


def _attn_kernel(lam_ref, cfar_ref, q_ref, k_ref, v_ref, bd_ref, bm_ref, g_ref, o_ref,
                 m_ref, l_ref, acc_ref, *, t, x_rows, out_scale):
    h = pl.program_id(0)
    i = pl.program_id(2)
    q = q_ref[0]
    lane = lax.broadcasted_iota(jnp.int32, q.shape, 1)
    zero = jnp.zeros_like(q)
    qs = (jnp.where(lane < HEAD_DIM, q, zero), jnp.where(lane >= HEAD_DIM, q, zero))
    c_far = cfar_ref[h]

    def scores(qm, kt):
        return lax.dot_general(qm, kt, (((1,), (1,)), ((), ())), preferred_element_type=F32)

    km = k_ref[0, x_rows:x_rows + META_BLOCK, :]
    vm = v_ref[0, x_rows:x_rows + META_BLOCK, :]
    bm = bm_ref[0, 0]
    for mp in range(2):
        s = scores(qs[mp], km) + bm
        m = jnp.max(s, axis=-1, keepdims=True)
        p = jnp.exp(s - m)
        m_ref[mp] = m
        l_ref[mp] = jnp.sum(p, axis=-1, keepdims=True)
        acc_ref[mp] = jnp.dot(p.astype(BF16), vm, preferred_element_type=F32)

    def update(mp, kt, vt, bias):
        s = scores(qs[mp], kt)
        m_old = m_ref[mp]
        if bias is None:
            m_new = jnp.maximum(m_old, jnp.max(s, axis=-1, keepdims=True) + c_far)
            p = jnp.exp(s - (m_new - c_far))
        else:
            s = s + bias
            m_new = jnp.maximum(m_old, jnp.max(s, axis=-1, keepdims=True))
            p = jnp.exp(s - m_new)
        alpha = jnp.exp(m_old - m_new)
        m_ref[mp] = m_new
        l_ref[mp] = alpha * l_ref[mp] + jnp.sum(p, axis=-1, keepdims=True)
        acc_ref[mp] = alpha * acc_ref[mp] + jnp.dot(p.astype(BF16), vt, preferred_element_type=F32)

    def far_body(j, carry):
        off = pl.multiple_of(j * t, t)
        kt = k_ref[0, pl.ds(off, t), :]
        vt = v_ref[0, pl.ds(off, t), :]
        update(0, kt, vt, None)
        update(1, kt, vt, None)
        return carry

    lax.fori_loop(0, jnp.maximum(i - 1, 0), far_body, 0)

    @pl.when(i >= 1)
    def _():
        off = pl.multiple_of((i - 1) * t, t)
        kt = k_ref[0, pl.ds(off, t), :]
        vt = v_ref[0, pl.ds(off, t), :]
        update(0, kt, vt, bd_ref[0, 1])
        update(1, kt, vt, bd_ref[0, 1])

    off = pl.multiple_of(i * t, t)
    kt = k_ref[0, pl.ds(off, t), :]
    vt = v_ref[0, pl.ds(off, t), :]
    update(0, kt, vt, bd_ref[0, 0])
    update(1, kt, vt, bd_ref[0, 0])

    o = acc_ref[0] * (1.0 / l_ref[0]) - lam_ref[0] * (acc_ref[1] * (1.0 / l_ref[1]))
    ms = jnp.mean(o * o, axis=-1, keepdims=True)
    o_ref[0] = (o * lax.rsqrt(ms + EPS) * (g_ref[...] * out_scale)).astype(o_ref.dtype)


def _attn_tile(s):
    return _pick(s, (512, 256, 128))


def _attention(qn, kn, proj, v_col, rel_bias, lam, subln_g, out_scale):
    bsz, s, width = qn.shape
    n_heads = width // V_DIM
    t = _attn_tile(s)
    lx = s + META_BLOCK
    bias_hd = rel_bias[_t5_bucket(jnp.arange(2 * t))].T.astype(F32)
    r = jnp.arange(t)[:, None]
    c = jnp.arange(t)[None, :]
    diag = jnp.where(r >= c, bias_hd[:, jnp.maximum(r - c, 0)], NEG_INF)
    sub = bias_hd[:, t + r - c]
    bias_d = jnp.stack([diag, sub], axis=1)
    mcol = jnp.arange(META_BLOCK)[None, :]
    dist_m = N_META + r - jnp.minimum(mcol, N_META - 1)
    meta0 = jnp.where(mcol < N_META, bias_hd[:, dist_m], NEG_INF)
    c_far = rel_bias[N_BUCKETS - 1].astype(F32)
    meta1 = jnp.where(mcol < N_META, c_far[:, None, None], NEG_INF) + jnp.zeros((1, t, 1), F32)
    bias_m = jnp.stack([meta0, meta1], axis=1)
    smem = pl.BlockSpec(memory_space=pltpu.SMEM)
    return pl.pallas_call(
        functools.partial(_attn_kernel, t=t, x_rows=s, out_scale=out_scale),
        grid=(n_heads, bsz, s // t),
        in_specs=[
            smem,
            smem,
            pl.BlockSpec((1, t, V_DIM), lambda h, b, i: (b, i, h)),
            pl.BlockSpec((1, lx, V_DIM), lambda h, b, i: (b, 0, h)),
            pl.BlockSpec((1, lx, V_DIM), lambda h, b, i: (b, 0, v_col // V_DIM + h)),
            pl.BlockSpec((1, 2, t, t), lambda h, b, i: (h, 0, 0, 0)),
            pl.BlockSpec((1, 1, t, META_BLOCK), lambda h, b, i: (h, jnp.minimum(i, 1), 0, 0)),
            pl.BlockSpec((1, V_DIM), lambda h, b, i: (0, 0)),
        ],
        out_specs=pl.BlockSpec((1, t, V_DIM), lambda h, b, i: (b, i, h)),
        out_shape=jax.ShapeDtypeStruct((bsz, s, width), BF16),
        scratch_shapes=[
            pltpu.VMEM((2, t, 1), F32),
            pltpu.VMEM((2, t, 1), F32),
            pltpu.VMEM((2, t, V_DIM), F32),
        ],
        compiler_params=_params("parallel", "parallel", "arbitrary"),
        name="diff_attn",
    )(lam.reshape(1).astype(F32), c_far, qn, kn, proj, bias_d, bias_m, subln_g.astype(F32).reshape(1, V_DIM))


def _gelu_tanh(x):
    return 0.5 * x * (1.0 + jnp.tanh(math.sqrt(2.0 / math.pi) * (x + 0.044715 * (x * x * x))))


def _ssm_kernel(uv_ref, wcat_ref, cc_ref, d_ref, ar_ref, ax_ref, as_ref, o_ref, z_ref, sp_ref, *, n_chunks):
    gb = uv_ref.shape[0]
    w = SSM_CHUNK * SSM_GROUP
    half = w // 2
    for g in range(gb):
        z_ref[g] = jnp.dot(uv_ref[g], wcat_ref[g], preferred_element_type=F32)

    ar = [ar_ref[g] for g in range(gb)]
    ax = [ax_ref[g] for g in range(gb)]
    asw = [as_ref[g] for g in range(gb)]

    def step(c, carry):
        rows = pl.ds(pl.multiple_of(c * SSM_ROWS, SSM_ROWS), SSM_ROWS)
        out = []
        for g in range(gb):
            x, xs = carry[g]
            sp_ref[g, rows, :] = x
            loc = z_ref[g, rows, w:w + half]
            loc_s = z_ref[g, rows, w + half:]
            out.append((ar[g] * x + ax[g] * xs + loc, ar[g] * xs + asw[g] * x + loc_s))
        return tuple(out)

    zero = jnp.zeros((SSM_ROWS, half), F32)
    lax.fori_loop(0, n_chunks, step, tuple((zero, zero) for _ in range(gb)))

    for g in range(gb):
        y = z_ref[g, :, :w] + jnp.dot(sp_ref[g].astype(BF16), cc_ref[g], preferred_element_type=F32)
        y = y + d_ref[g] * uv_ref[g].astype(F32)
        o_ref[g] = _gelu_tanh(y).astype(o_ref.dtype)


def _ssm_tables(lam_re, lam_im, log_dt, b_re, b_im, c_re, c_im, d_skip):
    n_groups, n_state = lam_re.shape
    tc = SSM_CHUNK
    lam = lax.complex(lam_re.astype(F32), lam_im.astype(F32))
    dt = jnp.exp(log_dt.astype(F32))[:, None]
    a_bar = jnp.exp(lam * dt)
    b_bar = ((a_bar - 1.0) / lam)[..., None] * lax.complex(b_re.astype(F32), b_im.astype(F32))
    c_mat = lax.complex(c_re.astype(F32), c_im.astype(F32))
    steps = jnp.arange(tc + 1, dtype=F32)
    a_pow = jnp.exp((lam * dt)[:, None, :] * steps[None, :, None])
    kern = jnp.real(jnp.einsum('gop,gdp,gpi->gdoi', c_mat, a_pow[:, :tc], b_bar))
    s_idx = jnp.arange(tc)[:, None]
    t_idx = jnp.arange(tc)[None, :]
    toe = kern[:, jnp.maximum(t_idx - s_idx, 0)]
    toe = jnp.where((t_idx >= s_idx)[None, :, :, None, None], toe, 0.0)
    m_intra = jnp.transpose(toe, (0, 1, 4, 2, 3)).reshape(n_groups, tc * SSM_GROUP, tc * SSM_GROUP)
    w_st = jnp.einsum('gsp,gpi->gsip', a_pow[:, tc - 1::-1][:, :tc], b_bar)
    w_st = w_st.reshape(n_groups, tc * SSM_GROUP, n_state)
    w_cat = jnp.concatenate([m_intra, jnp.real(w_st), jnp.imag(w_st), jnp.imag(w_st), jnp.real(w_st)], axis=-1)
    cc = jnp.einsum('gop,gtp->gpto', c_mat, a_pow[:, 1:])
    cc = cc.reshape(n_groups, n_state, tc * SSM_GROUP)
    c_carry = jnp.concatenate([jnp.real(cc), -jnp.imag(cc)], axis=1)
    a_end = a_pow[:, tc]
    ar = jnp.concatenate([jnp.real(a_end), jnp.real(a_end)], axis=-1)[:, None, :]
    ax = jnp.concatenate([-jnp.imag(a_end), jnp.imag(a_end)], axis=-1)[:, None, :]
    d_vec = jnp.tile(d_skip.astype(F32).reshape(n_groups, 1, SSM_GROUP), (1, 1, tc))
    return w_cat.astype(BF16), c_carry.astype(BF16), d_vec, ar, ax, -ax


def _ssm_scan(u_meta, u_x, tables):
    w_cat, c_carry, d_vec, ar, ax, asw = tables
    bsz, s, d_ssm = u_x.shape
    n_groups = d_ssm // SSM_GROUP
    tc = SSM_CHUNK
    u = jnp.concatenate([u_meta, u_x], axis=1)
    n_chunks = u.shape[1] // tc
    uv = u.reshape(bsz, n_chunks, tc, n_groups, SSM_GROUP)
    uv = jnp.transpose(uv, (3, 1, 0, 2, 4))
    uv = jnp.pad(uv, ((0, 0), (0, 0), (0, SSM_ROWS - bsz), (0, 0), (0, 0)))
    rows = n_chunks * SSM_ROWS
    w = tc * SSM_GROUP
    uv = uv.reshape(n_groups, rows, w)
    gb = _pick(n_groups, (4, 2, 1))
    n_state2 = c_carry.shape[1]
    grp = lambda *shape: pl.BlockSpec((gb,) + shape, lambda g: (g,) + (0,) * len(shape))
    out = pl.pallas_call(
        functools.partial(_ssm_kernel, n_chunks=n_chunks),
        grid=(n_groups // gb,),
        in_specs=[grp(rows, w), grp(w, w + 2 * n_state2), grp(n_state2, w), grp(1, w),
                  grp(1, n_state2), grp(1, n_state2), grp(1, n_state2)],
        out_specs=grp(rows, w),
        out_shape=jax.ShapeDtypeStruct((n_groups, rows, w), BF16),
        scratch_shapes=[pltpu.VMEM((gb, rows, w + 2 * n_state2), F32), pltpu.VMEM((gb, rows, n_state2), F32)],
        compiler_params=_params("parallel"),
        name="ssm_scan",
    )(uv, w_cat, c_carry, d_vec, ar, ax, asw)
    out = out.reshape(n_groups, n_chunks, SSM_ROWS, tc, SSM_GROUP)[:, 1:, :bsz]
    return jnp.transpose(out, (2, 1, 3, 0, 4)).reshape(bsz, s, d_ssm)


def _glu_kernel(g_ref, w_ref, b_ref, o_ref):
    g = g_ref[...]
    z = jnp.dot(g, w_ref[...], preferred_element_type=F32) + b_ref[...]
    o_ref[...] = (g.astype(F32) * _sigmoid(z)).astype(o_ref.dtype)


def _glu(g, w, b):
    m, k = g.shape
    tm = _pick(m, (1024, 512, 256, 128))
    return pl.pallas_call(
        _glu_kernel,
        grid=(m // tm,),
        in_specs=[pl.BlockSpec((tm, k), lambda i: (i, 0)), pl.BlockSpec((k, k), lambda i: (0, 0)),
                  pl.BlockSpec((1, k), lambda i: (0, 0))],
        out_specs=pl.BlockSpec((tm, k), lambda i: (i, 0)),
        out_shape=jax.ShapeDtypeStruct((m, k), BF16),
        compiler_params=_params("parallel"),
        name="glu",
    )(g, w, b.astype(F32).reshape(1, k))


def kernel(x, meta_tokens, rel_bias, ln1_g, w_in, q_norm_g, k_norm_g, lam_q1, lam_k1, lam_q2, lam_k2, subln_g, lam_re, lam_im, log_dt, b_re, b_im, c_re, c_im, d_skip, w_glu, b_glu, w_branch, w_o, ln2_g, w_gate_up, w_down):
    bsz, s, d = x.shape
    depth = w_in.shape[0]
    assert depth == 1, "queries, gates and the FFN are only evaluated for sequence tokens (single layer)"
    assert meta_tokens.shape[0] == N_META and s % META_BLOCK == 0
    d_ssm = w_glu.shape[1]
    d_att = w_branch.shape[1] - d_ssm
    qk_w = (d_att // V_DIM) * 2 * HEAD_DIM
    d_ff = w_down.shape[1]
    lx = s + META_BLOCK
    l = 0
    lambda_init = 0.8 - 0.6 * math.exp(-0.3 * l)

    q_col = d_ssm
    k_col = q_col + qk_w
    v_col = k_col + qk_w
    gs_col = v_col + d_att
    ga_col = gs_col + d

    hn = _ln1(x, meta_tokens.astype(F32), ln1_g[l].astype(F32))
    proj = _in_proj(hn.reshape(bsz * lx, d), w_in[l].astype(BF16)).reshape(bsz, lx, -1)

    qn = _qknorm(proj, q_col, qk_w, s, q_norm_g[l], HEAD_DIM ** -0.5)
    kn = _qknorm(proj, k_col, qk_w, lx, k_norm_g[l], 1.0)
    lam = (jnp.exp(jnp.sum(lam_q1[l].astype(F32) * lam_k1[l].astype(F32)))
           - jnp.exp(jnp.sum(lam_q2[l].astype(F32) * lam_k2[l].astype(F32))) + lambda_init)
    y_att = _attention(qn, kn, proj, v_col, rel_bias, lam, subln_g[l], 1.0 - lambda_init)

    tables = _ssm_tables(lam_re[l], lam_im[l], log_dt[l], b_re[l], b_im[l], c_re[l], c_im[l], d_skip[l])
    gel = _ssm_scan(proj[:, s:s + N_META, :d_ssm], proj[:, :s, :d_ssm], tables)
    y_ssm = _glu(gel.reshape(bsz * s, d_ssm), w_glu[l].astype(BF16), b_glu[l]).reshape(bsz, s, d_ssm)

    wb = w_branch[l].astype(BF16)
    merged = _merge(y_ssm, y_att, wb[:d_ssm], wb[d_ssm:], proj, gs_col, ga_col)
    h1 = _out_proj(merged.reshape(bsz * s, d), w_o[l].astype(BF16), x.reshape(bsz * s, d))

    hn2 = _ln2(h1, ln2_g[l].astype(F32))
    d_ffp = -(-d_ff // FF_ALIGN) * FF_ALIGN
    wgu = w_gate_up[l].astype(BF16)
    pad_c = ((0, 0), (0, d_ffp - d_ff))
    act = _swiglu(hn2, jnp.pad(wgu[:, :d_ff], pad_c), jnp.pad(wgu[:, d_ff:], pad_c))
    out = _down_proj(act, jnp.pad(w_down[l].astype(BF16), ((0, d_ffp - d_ff), (0, 0))), h1)
    return out.reshape(bsz, s, d)
```

```python
import functools
import math

import jax
import jax.numpy as jnp
from jax import lax
from jax.experimental import pallas as pl
from jax.experimental.pallas import tpu as pltpu

F32 = jnp.float32
BF16 = jnp.bfloat16

N_META = 16
META_BLOCK = 128
EPS = 1e-6
SSM_GROUP = 16
SSM_CHUNK = 16
SSM_ROWS = 8
HEAD_DIM = 64
V_DIM = 2 * HEAD_DIM
ATT_ROWS = 128
N_BUCKETS = 32
MAX_DISTANCE = 128
NEG_INF = -1e30
LOG2E = math.log2(math.e)
FF_ALIGN = 1024
VMEM_LIMIT = 56 * 1024 * 1024


def _params(*semantics):
    return pltpu.CompilerParams(dimension_semantics=semantics, vmem_limit_bytes=VMEM_LIMIT)


def _pick(n, candidates):
    for c in candidates:
        if n % c == 0:
            return c
    return n


def _ln1_kernel(x_ref, meta_ref, g_ref, o_ref, *, n_x_blocks):
    i = pl.program_id(1)
    g = g_ref[...]

    def norm(v):
        ms = jnp.mean(v * v, axis=-1, keepdims=True)
        return (v * lax.rsqrt(ms + EPS) * g).astype(o_ref.dtype)

    @pl.when(i < n_x_blocks)
    def _():
        o_ref[0] = norm(x_ref[0])

    @pl.when(i == n_x_blocks)
    def _():
        o_ref[0] = jnp.zeros(o_ref.shape[1:], o_ref.dtype)
        o_ref[0, :N_META, :] = norm(meta_ref[...])


def _ln1(x, meta, g):
    bsz, s, d = x.shape
    tr = META_BLOCK
    nxb = s // tr
    return pl.pallas_call(
        functools.partial(_ln1_kernel, n_x_blocks=nxb),
        grid=(bsz, nxb + 1),
        in_specs=[
            pl.BlockSpec((1, tr, d), lambda b, i: (b, jnp.minimum(i, nxb - 1), 0)),
            pl.BlockSpec((N_META, d), lambda b, i: (0, 0)),
            pl.BlockSpec((1, d), lambda b, i: (0, 0)),
        ],
        out_specs=pl.BlockSpec((1, tr, d), lambda b, i: (b, i, 0)),
        out_shape=jax.ShapeDtypeStruct((bsz, s + META_BLOCK, d), BF16),
        compiler_params=_params("parallel", "arbitrary"),
        name="ln1",
    )(x, meta, g.reshape(1, d))


def _ln2_kernel(x_ref, g_ref, o_ref):
    v = x_ref[...]
    ms = jnp.mean(v * v, axis=-1, keepdims=True)
    o_ref[...] = (v * lax.rsqrt(ms + EPS) * g_ref[...]).astype(o_ref.dtype)


def _ln2(h, g):
    m, d = h.shape
    tr = _pick(m, (256, 128))
    return pl.pallas_call(
        _ln2_kernel,
        grid=(m // tr,),
        in_specs=[pl.BlockSpec((tr, d), lambda i: (i, 0)), pl.BlockSpec((1, d), lambda i: (0, 0))],
        out_specs=pl.BlockSpec((tr, d), lambda i: (i, 0)),
        out_shape=jax.ShapeDtypeStruct((m, d), BF16),
        compiler_params=_params("parallel"),
        name="ln2",
    )(h, g.reshape(1, d))


def _mm_kernel(a_ref, w_ref, o_ref):
    o_ref[...] = jnp.dot(a_ref[...], w_ref[...], preferred_element_type=F32).astype(o_ref.dtype)


def _in_proj(hn, w):
    m, k = hn.shape
    n = w.shape[1]
    tm = _pick(m, (1056, 528, 136))
    tn = _pick(n, (1024, 512, 256, 128))
    return pl.pallas_call(
        _mm_kernel,
        grid=(m // tm, n // tn),
        in_specs=[pl.BlockSpec((tm, k), lambda i, j: (i, 0)), pl.BlockSpec((k, tn), lambda i, j: (0, j))],
        out_specs=pl.BlockSpec((tm, tn), lambda i, j: (i, j)),
        out_shape=jax.ShapeDtypeStruct((m, n), BF16),
        compiler_params=_params("parallel", "arbitrary"),
        name="in_proj",
    )(hn, w)


def _sigmoid(x):
    return 1.0 / (1.0 + jnp.exp(-x))


def _merge_kernel(ys_ref, ya_ref, wa_ref, wb_ref, gs_ref, ga_ref, o_ref):
    a = jnp.dot(ys_ref[0], wa_ref[...], preferred_element_type=F32)
    b = jnp.dot(ya_ref[0], wb_ref[...], preferred_element_type=F32)
    gs = _sigmoid(gs_ref[0].astype(F32))
    ga = _sigmoid(ga_ref[0].astype(F32))
    o_ref[0] = (gs * a + ga * b).astype(o_ref.dtype)


def _merge(y_ssm, y_att, wa, wb, proj, gs_col, ga_col):
    bsz, s, d_ssm = y_ssm.shape
    d_att = y_att.shape[2]
    d = wa.shape[1]
    tm = _pick(s, (1024, 512, 256, 128))
    tn = _pick(math.gcd(math.gcd(d, gs_col), ga_col), (512, 256, 128))
    return pl.pallas_call(
        _merge_kernel,
        grid=(bsz, s // tm, d // tn),
        in_specs=[
            pl.BlockSpec((1, tm, d_ssm), lambda b, i, j: (b, i, 0)),
            pl.BlockSpec((1, tm, d_att), lambda b, i, j: (b, i, 0)),
            pl.BlockSpec((d_ssm, tn), lambda b, i, j: (0, j)),
            pl.BlockSpec((d_att, tn), lambda b, i, j: (0, j)),
            pl.BlockSpec((1, tm, tn), lambda b, i, j: (b, i, gs_col // tn + j)),
            pl.BlockSpec((1, tm, tn), lambda b, i, j: (b, i, ga_col // tn + j)),
        ],
        out_specs=pl.BlockSpec((1, tm, tn), lambda b, i, j: (b, i, j)),
        out_shape=jax.ShapeDtypeStruct((bsz, s, d), BF16),
        compiler_params=_params("parallel", "parallel", "arbitrary"),
        name="merge",
    )(y_ssm, y_att, wa, wb, proj, proj)


def _resid_mm_kernel(a_ref, w_ref, r_ref, o_ref):
    o_ref[...] = r_ref[...] + jnp.dot(a_ref[...], w_ref[...], preferred_element_type=F32)


def _out_proj(a, w, resid):
    m, k = a.shape
    n = w.shape[1]
    tm = _pick(m, (1024, 512, 256, 128))
    tn = _pick(n, (512, 256, 128))
    return pl.pallas_call(
        _resid_mm_kernel,
        grid=(m // tm, n // tn),
        in_specs=[
            pl.BlockSpec((tm, k), lambda i, j: (i, 0)),
            pl.BlockSpec((k, tn), lambda i, j: (0, j)),
            pl.BlockSpec((tm, tn), lambda i, j: (i, j)),
        ],
        out_specs=pl.BlockSpec((tm, tn), lambda i, j: (i, j)),
        out_shape=jax.ShapeDtypeStruct((m, n), F32),
        compiler_params=_params("parallel", "arbitrary"),
        name="out_proj",
    )(a, w, resid)


def _swiglu_kernel(a_ref, wg_ref, wu_ref, o_ref):
    a = a_ref[...]
    g = jnp.dot(a, wg_ref[...], preferred_element_type=F32)
    u = jnp.dot(a, wu_ref[...], preferred_element_type=F32)
    o_ref[...] = (g * _sigmoid(g) * u).astype(o_ref.dtype)


def _swiglu(a, wg, wu):
    m, k = a.shape
    n = wg.shape[1]
    tm = _pick(m, (1024, 512, 256, 128))
    tn = _pick(n, (512, 256, 128))
    return pl.pallas_call(
        _swiglu_kernel,
        grid=(m // tm, n // tn),
        in_specs=[
            pl.BlockSpec((tm, k), lambda i, j: (i, 0)),
            pl.BlockSpec((k, tn), lambda i, j: (0, j)),
            pl.BlockSpec((k, tn), lambda i, j: (0, j)),
        ],
        out_specs=pl.BlockSpec((tm, tn), lambda i, j: (i, j)),
        out_shape=jax.ShapeDtypeStruct((m, n), BF16),
        compiler_params=_params("parallel", "arbitrary"),
        name="swiglu",
    )(a, wg, wu)


def _down_kernel(a_ref, w_ref, r_ref, o_ref, acc_ref):
    kk = pl.program_id(2)

    @pl.when(kk == 0)
    def _():
        acc_ref[...] = r_ref[...]

    acc_ref[...] += jnp.dot(a_ref[...], w_ref[...], preferred_element_type=F32)

    @pl.when(kk == pl.num_programs(2) - 1)
    def _():
        o_ref[...] = acc_ref[...]


def _down_proj(a, w, resid):
    m, k = a.shape
    n = w.shape[1]
    tm = _pick(m, (1024, 512, 256, 128))
    tn = _pick(n, (1024, 512, 256, 128))
    tk = _pick(k, (2816, 1024, 512, 256, 128))
    return pl.pallas_call(
        _down_kernel,
        grid=(m // tm, n // tn, k // tk),
        in_specs=[
            pl.BlockSpec((tm, tk), lambda i, j, kk: (i, kk)),
            pl.BlockSpec((tk, tn), lambda i, j, kk: (kk, j)),
            pl.BlockSpec((tm, tn), lambda i, j, kk: (i, j)),
        ],
        out_specs=pl.BlockSpec((tm, tn), lambda i, j, kk: (i, j)),
        out_shape=jax.ShapeDtypeStruct((m, n), F32),
        scratch_shapes=[pltpu.VMEM((tm, tn), F32)],
        compiler_params=_params("parallel", "parallel", "arbitrary"),
        name="down_proj",
    )(a, w, resid)


def _qknorm_kernel(x_ref, g_ref, gm_ref, o_ref, *, scale):
    v = x_ref[0].astype(F32)
    sq = (v * v).astype(BF16)
    gm = gm_ref[...]
    w = gm.shape[0]
    ss = jnp.concatenate(
        [jnp.dot(sq[:, c:c + w], gm, preferred_element_type=F32) for c in range(0, v.shape[1], w)], axis=1)
    o_ref[0] = (v * lax.rsqrt(ss * (1.0 / HEAD_DIM) + EPS) * (g_ref[...] * scale)).astype(o_ref.dtype)


def _qknorm(proj, col0, width, rows, gain, scale):
    bsz = proj.shape[0]
    tr = META_BLOCK
    tc = _pick(math.gcd(col0, width), (1024, 512, 256))
    gw = min(256, tc)
    idx = jnp.arange(gw) // HEAD_DIM
    gmat = (idx[:, None] == idx[None, :]).astype(BF16)
    gain_t = jnp.tile(gain.astype(F32), tc // HEAD_DIM).reshape(1, tc)
    return pl.pallas_call(
        functools.partial(_qknorm_kernel, scale=scale),
        grid=(bsz, rows // tr, width // tc),
        in_specs=[
            pl.BlockSpec((1, tr, tc), lambda b, i, c: (b, i, col0 // tc + c)),
            pl.BlockSpec((1, tc), lambda b, i, c: (0, 0)),
            pl.BlockSpec((gw, gw), lambda b, i, c: (0, 0)),
        ],
        out_specs=pl.BlockSpec((1, tr, tc), lambda b, i, c: (b, i, c)),
        out_shape=jax.ShapeDtypeStruct((bsz, rows, width), BF16),
        compiler_params=_params("parallel", "parallel", "arbitrary"),
        name="qknorm",
    )(proj, gain_t, gmat)


def _t5_bucket(n):
    max_exact = N_BUCKETS // 2
    nf = jnp.maximum(n, max_exact).astype(F32)
    log_b = max_exact + (jnp.log(nf / max_exact) / math.log(MAX_DISTANCE / max_exact)
                         * (N_BUCKETS - max_exact)).astype(jnp.int32)
    return jnp.where(n < max_exact, n, jnp.minimum(log_b, N_BUCKETS - 1))


def _attn_kernel(lam_ref, cfar_ref, q_ref, k_ref, v_ref, bd_ref, bm_ref, g_ref, o_ref,
                 m_ref, l_ref, acc_ref, s_ref, mx_ref, *, t, x_rows, out_scale):
    h = pl.program_id(0)
    i = pl.program_id(2)
    q = q_ref[0]
    lane = lax.broadcasted_iota(jnp.int32, q.shape, 1)
    zero = jnp.zeros_like(q)
    qs = (jnp.where(lane < HEAD_DIM, q, zero), jnp.where(lane >= HEAD_DIM, q, zero))
    c_far = cfar_ref[h]
    rc = min(ATT_ROWS, t)
    n_rc = t // rc

    def lanes(v, n):
        return v if n == V_DIM else jnp.concatenate([v] * (n // V_DIM), axis=1)

    def scores(mp, rows, kt, bias):
        n = kt.shape[0]
        s = lax.dot_general(qs[mp][rows], kt, (((1,), (1,)), ((), ())), preferred_element_type=F32)
        if bias is not None:
            s = s + bias
        s_ref[mp, rows, :n] = s
        mx = jnp.max(s, axis=-1, keepdims=True)
        mx_ref[mp, rows, :] = jnp.broadcast_to(mx, (mx.shape[0], V_DIM))

    def chunk(mp, c, vt, far, first=False):
        rows = slice(c * rc, (c + 1) * rc)
        n = vt.shape[0]
        mx = mx_ref[mp, rows, :]
        if far:
            mx = mx + c_far
        if first:
            m_new = mx
        else:
            m_old = m_ref[mp, rows, :]
            m_new = jnp.maximum(m_old, mx)
        p = jnp.exp2(s_ref[mp, rows, :n] - lanes(m_new - c_far if far else m_new, n))
        ps = jnp.sum(p, axis=-1, keepdims=True)
        pv = jnp.dot(p.astype(BF16), vt, preferred_element_type=F32)
        if first:
            l_ref[mp, rows, :] = jnp.broadcast_to(ps, (rc, V_DIM))
            acc_ref[mp, rows, :] = pv
        else:
            alpha = jnp.exp2(m_old - m_new)
            l_ref[mp, rows, :] = alpha * l_ref[mp, rows, :] + ps
            acc_ref[mp, rows, :] = alpha * acc_ref[mp, rows, :] + pv
        m_ref[mp, rows, :] = m_new

    def full_tile(kt, vt, bias, first=False):
        for mp in range(2):
            scores(mp, slice(0, t), kt, bias)
        for c in range(n_rc):
            for mp in range(2):
                chunk(mp, c, vt, bias is None, first)

    full_tile(k_ref[0, x_rows:x_rows + META_BLOCK, :], v_ref[0, x_rows:x_rows + META_BLOCK, :],
              bm_ref[0, 0], first=True)

    def far_body(j, carry):
        off = pl.multiple_of(j * t, t)
        full_tile(k_ref[0, pl.ds(off, t), :], v_ref[0, pl.ds(off, t), :], None)
        return carry

    lax.fori_loop(0, jnp.maximum(i - 1, 0), far_body, 0)

    @pl.when(i >= 1)
    def _():
        off = pl.multiple_of((i - 1) * t, t)
        full_tile(k_ref[0, pl.ds(off, t), :], v_ref[0, pl.ds(off, t), :], bd_ref[0, 1])

    off = pl.multiple_of(i * t, t)
    n_keys = lambda c: -(-((c + 1) * rc) // V_DIM) * V_DIM
    for c in range(n_rc):
        rows = slice(c * rc, (c + 1) * rc)
        for mp in range(2):
            scores(mp, rows, k_ref[0, pl.ds(off, n_keys(c)), :], bd_ref[0, 0, rows, :n_keys(c)])
    for c in range(n_rc):
        vt = v_ref[0, pl.ds(off, n_keys(c)), :]
        for mp in range(2):
            chunk(mp, c, vt, False)

    o = acc_ref[0] * (1.0 / l_ref[0]) - lam_ref[0] * (acc_ref[1] * (1.0 / l_ref[1]))
    ms = jnp.mean(o * o, axis=-1, keepdims=True)
    o_ref[0] = (o * lax.rsqrt(ms + EPS) * (g_ref[...] * out_scale)).astype(o_ref.dtype)


def _attn_tile(s):
    return _pick(s, (512, 256, 128))


def _attention(qn, kn, proj, v_col, rel_bias, lam, subln_g, out_scale):
    bsz, s, width = qn.shape
    n_heads = width // V_DIM
    t = _attn_tile(s)
    lx = s + META_BLOCK
    rel_bias = rel_bias.astype(F32) * LOG2E
    bias_hd = rel_bias[_t5_bucket(jnp.arange(2 * t))].T
    masked = jnp.full((n_heads, 1), NEG_INF, F32)

    def toeplitz(f, ncols):
        flat = jnp.tile(f, (1, t))[:, :t * (2 * t - 1)]
        return flat.reshape(n_heads, t, 2 * t - 1)[:, :, :ncols]

    rev = lambda lo, hi: bias_hd[:, hi:lo:-1]
    diag = toeplitz(jnp.concatenate([bias_hd[:, :1], jnp.tile(masked, (1, t)), rev(0, t - 1)], axis=1), t)
    sub = toeplitz(jnp.concatenate([rev(0, t), masked, rev(t, 2 * t - 1)], axis=1), t)
    bias_d = jnp.stack([diag, sub], axis=1)
    meta0 = toeplitz(jnp.concatenate([rev(0, N_META), jnp.tile(masked, (1, t - N_META + 1)),
                                      rev(N_META, t + N_META - 1)], axis=1), N_META)
    c_far = rel_bias[N_BUCKETS - 1].astype(F32)
    meta1 = jnp.broadcast_to(c_far[:, None, None], (n_heads, t, N_META))
    bias_m = jnp.pad(jnp.stack([meta0, meta1], axis=1), ((0, 0), (0, 0), (0, 0), (0, META_BLOCK - N_META)),
                     constant_values=NEG_INF)
    smem = pl.BlockSpec(memory_space=pltpu.SMEM)
    return pl.pallas_call(
        functools.partial(_attn_kernel, t=t, x_rows=s, out_scale=out_scale),
        grid=(n_heads, bsz, s // t),
        in_specs=[
            smem,
            smem,
            pl.BlockSpec((1, t, V_DIM), lambda h, b, i: (b, i, h)),
            pl.BlockSpec((1, lx, V_DIM), lambda h, b, i: (b, 0, h)),
            pl.BlockSpec((1, lx, V_DIM), lambda h, b, i: (b, 0, v_col // V_DIM + h)),
            pl.BlockSpec((1, 2, t, t), lambda h, b, i: (h, 0, 0, 0)),
            pl.BlockSpec((1, 1, t, META_BLOCK), lambda h, b, i: (h, jnp.minimum(i, 1), 0, 0)),
            pl.BlockSpec((1, V_DIM), lambda h, b, i: (0, 0)),
        ],
        out_specs=pl.BlockSpec((1, t, V_DIM), lambda h, b, i: (b, i, h)),
        out_shape=jax.ShapeDtypeStruct((bsz, s, width), BF16),
        scratch_shapes=[
            pltpu.VMEM((2, t, V_DIM), F32),
            pltpu.VMEM((2, t, V_DIM), F32),
            pltpu.VMEM((2, t, V_DIM), F32),
            pltpu.VMEM((2, t, t), F32),
            pltpu.VMEM((2, t, V_DIM), F32),
        ],
        compiler_params=_params("parallel", "parallel", "arbitrary"),
        name="diff_attn",
    )(lam.reshape(1).astype(F32), c_far, qn, kn, proj, bias_d, bias_m, subln_g.astype(F32).reshape(1, V_DIM))


def _gelu_tanh(x):
    return 0.5 * x * (1.0 + jnp.tanh(math.sqrt(2.0 / math.pi) * (x + 0.044715 * (x * x * x))))


def _ssm_kernel(uv_ref, wcat_ref, cc_ref, d_ref, ar_ref, ax_ref, as_ref, o_ref, z_ref, sp_ref, *, n_chunks):
    gb = uv_ref.shape[0]
    w = SSM_CHUNK * SSM_GROUP
    half = w // 2
    for g in range(gb):
        z_ref[g] = jnp.dot(uv_ref[g], wcat_ref[g], preferred_element_type=F32)

    ar = [ar_ref[g] for g in range(gb)]
    ax = [ax_ref[g] for g in range(gb)]
    asw = [as_ref[g] for g in range(gb)]

    def step(c, carry):
        rows = pl.ds(pl.multiple_of(c * SSM_ROWS, SSM_ROWS), SSM_ROWS)
        out = []
        for g in range(gb):
            x, xs = carry[g]
            sp_ref[g, rows, :] = x
            loc = z_ref[g, rows, w:w + half]
            loc_s = z_ref[g, rows, w + half:]
            out.append((ar[g] * x + ax[g] * xs + loc, ar[g] * xs + asw[g] * x + loc_s))
        return tuple(out)

    zero = jnp.zeros((SSM_ROWS, half), F32)
    lax.fori_loop(0, n_chunks, step, tuple((zero, zero) for _ in range(gb)))

    for g in range(gb):
        y = z_ref[g, :, :w] + jnp.dot(sp_ref[g].astype(BF16), cc_ref[g], preferred_element_type=F32)
        y = y + d_ref[g] * uv_ref[g].astype(F32)
        o_ref[g] = _gelu_tanh(y).astype(o_ref.dtype)


def _ssm_tables(lam_re, lam_im, log_dt, b_re, b_im, c_re, c_im, d_skip):
    n_groups, n_state = lam_re.shape
    tc = SSM_CHUNK
    lam = lax.complex(lam_re.astype(F32), lam_im.astype(F32))
    dt = jnp.exp(log_dt.astype(F32))[:, None]
    a_bar = jnp.exp(lam * dt)
    b_bar = ((a_bar - 1.0) / lam)[..., None] * lax.complex(b_re.astype(F32), b_im.astype(F32))
    c_mat = lax.complex(c_re.astype(F32), c_im.astype(F32))
    steps = jnp.arange(tc + 1, dtype=F32)
    a_pow = jnp.exp((lam * dt)[:, None, :] * steps[None, :, None])
    kern = jnp.real(jnp.einsum('gop,gdp,gpi->gdoi', c_mat, a_pow[:, :tc], b_bar))
    s_idx = jnp.arange(tc)[:, None]
    t_idx = jnp.arange(tc)[None, :]
    toe = kern[:, jnp.maximum(t_idx - s_idx, 0)]
    toe = jnp.where((t_idx >= s_idx)[None, :, :, None, None], toe, 0.0)
    m_intra = jnp.transpose(toe, (0, 1, 4, 2, 3)).reshape(n_groups, tc * SSM_GROUP, tc * SSM_GROUP)
    w_st = jnp.einsum('gsp,gpi->gsip', a_pow[:, tc - 1::-1][:, :tc], b_bar)
    w_st = w_st.reshape(n_groups, tc * SSM_GROUP, n_state)
    w_cat = jnp.concatenate([m_intra, jnp.real(w_st), jnp.imag(w_st), jnp.imag(w_st), jnp.real(w_st)], axis=-1)
    cc = jnp.einsum('gop,gtp->gpto', c_mat, a_pow[:, 1:])
    cc = cc.reshape(n_groups, n_state, tc * SSM_GROUP)
    c_carry = jnp.concatenate([jnp.real(cc), -jnp.imag(cc)], axis=1)
    a_end = a_pow[:, tc]
    ar = jnp.concatenate([jnp.real(a_end), jnp.real(a_end)], axis=-1)[:, None, :]
    ax = jnp.concatenate([-jnp.imag(a_end), jnp.imag(a_end)], axis=-1)[:, None, :]
    d_vec = jnp.tile(d_skip.astype(F32).reshape(n_groups, 1, SSM_GROUP), (1, 1, tc))
    return w_cat.astype(BF16), c_carry.astype(BF16), d_vec, ar, ax, -ax


def _ssm_scan(u_meta, u_x, tables):
    w_cat, c_carry, d_vec, ar, ax, asw = tables
    bsz, s, d_ssm = u_x.shape
    n_groups = d_ssm // SSM_GROUP
    tc = SSM_CHUNK
    u = jnp.concatenate([u_meta, u_x], axis=1)
    n_chunks = u.shape[1] // tc
    uv = u.reshape(bsz, n_chunks, tc, n_groups, SSM_GROUP)
    uv = jnp.transpose(uv, (3, 1, 0, 2, 4))
    uv = jnp.pad(uv, ((0, 0), (0, 0), (0, SSM_ROWS - bsz), (0, 0), (0, 0)))
    rows = n_chunks * SSM_ROWS
    w = tc * SSM_GROUP
    uv = uv.reshape(n_groups, rows, w)
    gb = _pick(n_groups, (4, 2, 1))
    n_state2 = c_carry.shape[1]
    grp = lambda *shape: pl.BlockSpec((gb,) + shape, lambda g: (g,) + (0,) * len(shape))
    out = pl.pallas_call(
        functools.partial(_ssm_kernel, n_chunks=n_chunks),
        grid=(n_groups // gb,),
        in_specs=[grp(rows, w), grp(w, w + 2 * n_state2), grp(n_state2, w), grp(1, w),
                  grp(1, n_state2), grp(1, n_state2), grp(1, n_state2)],
        out_specs=grp(rows, w),
        out_shape=jax.ShapeDtypeStruct((n_groups, rows, w), BF16),
        scratch_shapes=[pltpu.VMEM((gb, rows, w + 2 * n_state2), F32), pltpu.VMEM((gb, rows, n_state2), F32)],
        compiler_params=_params("parallel"),
        name="ssm_scan",
    )(uv, w_cat, c_carry, d_vec, ar, ax, asw)
    out = out.reshape(n_groups, n_chunks, SSM_ROWS, tc, SSM_GROUP)[:, 1:, :bsz]
    return jnp.transpose(out, (2, 1, 3, 0, 4)).reshape(bsz, s, d_ssm)


def _glu_kernel(g_ref, w_ref, b_ref, o_ref):
    g = g_ref[...]
    z = jnp.dot(g, w_ref[...], preferred_element_type=F32) + b_ref[...]
    o_ref[...] = (g.astype(F32) * _sigmoid(z)).astype(o_ref.dtype)


def _glu(g, w, b):
    m, k = g.shape
    tm = _pick(m, (1024, 512, 256, 128))
    return pl.pallas_call(
        _glu_kernel,
        grid=(m // tm,),
        in_specs=[pl.BlockSpec((tm, k), lambda i: (i, 0)), pl.BlockSpec((k, k), lambda i: (0, 0)),
                  pl.BlockSpec((1, k), lambda i: (0, 0))],
        out_specs=pl.BlockSpec((tm, k), lambda i: (i, 0)),
        out_shape=jax.ShapeDtypeStruct((m, k), BF16),
        compiler_params=_params("parallel"),
        name="glu",
    )(g, w, b.astype(F32).reshape(1, k))


def kernel(x, meta_tokens, rel_bias, ln1_g, w_in, q_norm_g, k_norm_g, lam_q1, lam_k1, lam_q2, lam_k2, subln_g, lam_re, lam_im, log_dt, b_re, b_im, c_re, c_im, d_skip, w_glu, b_glu, w_branch, w_o, ln2_g, w_gate_up, w_down):
    bsz, s, d = x.shape
    depth = w_in.shape[0]
    assert depth == 1, "queries, gates and the FFN are only evaluated for sequence tokens (single layer)"
    assert meta_tokens.shape[0] == N_META and s % META_BLOCK == 0
    d_ssm = w_glu.shape[1]
    d_att = w_branch.shape[1] - d_ssm
    qk_w = (d_att // V_DIM) * 2 * HEAD_DIM
    d_ff = w_down.shape[1]
    lx = s + META_BLOCK
    l = 0
    lambda_init = 0.8 - 0.6 * math.exp(-0.3 * l)

    q_col = d_ssm
    k_col = q_col + qk_w
    v_col = k_col + qk_w
    gs_col = v_col + d_att
    ga_col = gs_col + d

    hn = _ln1(x, meta_tokens.astype(F32), ln1_g[l].astype(F32))
    proj = _in_proj(hn.reshape(bsz * lx, d), w_in[l].astype(BF16)).reshape(bsz, lx, -1)

    qn = _qknorm(proj, q_col, qk_w, s, q_norm_g[l], HEAD_DIM ** -0.5 * LOG2E)
    kn = _qknorm(proj, k_col, qk_w, lx, k_norm_g[l], 1.0)
    lam = (jnp.exp(jnp.sum(lam_q1[l].astype(F32) * lam_k1[l].astype(F32)))
           - jnp.exp(jnp.sum(lam_q2[l].astype(F32) * lam_k2[l].astype(F32))) + lambda_init)
    y_att = _attention(qn, kn, proj, v_col, rel_bias, lam, subln_g[l], 1.0 - lambda_init)

    tables = _ssm_tables(lam_re[l], lam_im[l], log_dt[l], b_re[l], b_im[l], c_re[l], c_im[l], d_skip[l])
    gel = _ssm_scan(proj[:, s:s + N_META, :d_ssm], proj[:, :s, :d_ssm], tables)
    y_ssm = _glu(gel.reshape(bsz * s, d_ssm), w_glu[l].astype(BF16), b_glu[l]).reshape(bsz, s, d_ssm)

    wb = w_branch[l].astype(BF16)
    merged = _merge(y_ssm, y_att, wb[:d_ssm], wb[d_ssm:], proj, gs_col, ga_col)
    h1 = _out_proj(merged.reshape(bsz * s, d), w_o[l].astype(BF16), x.reshape(bsz * s, d))

    hn2 = _ln2(h1, ln2_g[l].astype(F32))
    d_ffp = -(-d_ff // FF_ALIGN) * FF_ALIGN
    wgu = w_gate_up[l].astype(BF16)
    pad_c = ((0, 0), (0, d_ffp - d_ff))
    act = _swiglu(hn2, jnp.pad(wgu[:, :d_ff], pad_c), jnp.pad(wgu[:, d_ff:], pad_c))
    out = _down_proj(act, jnp.pad(w_down[l].astype(BF16), ((0, d_ffp - d_ff), (0, 0))), h1)
    return out.reshape(bsz, s, d)
```
